```python
import math, functools
import jax, jax.numpy as jnp
from jax import lax
import numpy as np

D_MODEL = 1024
BATCH = 2
SEQ = 8192
DEPTH = 1
DEC_BATCH = 128
DEC_SEQ = 4
PAST_LEN = 8192
PAGE_SIZE = 128

GDN_HEADS = 4
GDN_DK = D_MODEL // 8
GDN_DV = D_MODEL // 8
GDN_CONV = 4
GDN_CHUNK = 64
GDN_QK = GDN_HEADS * GDN_DK
GDN_V = GDN_HEADS * GDN_DV
GDN_CONV_DIM = 2 * GDN_QK + GDN_V
DIFF_HEADS = 4
DIFF_D = D_MODEL // 16
DIFF_DV = 2 * DIFF_D
DIFF_QK = DIFF_HEADS * 2 * DIFF_D
DIFF_V = DIFF_HEADS * DIFF_DV
ROPE_DIM = DIFF_D // 4
ROPE_THETA = 500000.0
Q_BLOCK = 128
SPLIT_Z = GDN_CONV_DIM
SPLIT_B = SPLIT_Z + GDN_V
SPLIT_A = SPLIT_B + GDN_HEADS
SPLIT_DQ = SPLIT_A + GDN_HEADS
SPLIT_DK = SPLIT_DQ + DIFF_QK
SPLIT_DV = SPLIT_DK + DIFF_QK
IN_DIM = SPLIT_DV + DIFF_V
MIX_WIDTH = GDN_V + DIFF_V
FFN_DIM = ((8 * D_MODEL // 3 + 255) // 256) * 256
MACARON_W = 0.5
N_MOD = 9
NORM_EPS = 1e-6
L2_EPS = 1e-6
NEG_INF = -1e30

kernel_name = 'gdn_diffattn_macaron_step'


def rmsnorm(x, g):
    xf = x.astype(jnp.float32)
    y = xf * lax.rsqrt(jnp.mean(xf * xf, axis=-1, keepdims=True) + NORM_EPS)
    return (y * g.astype(jnp.float32)).astype(x.dtype)


def l2norm(x):
    xf = x.astype(jnp.float32)
    return xf * lax.rsqrt(jnp.sum(xf * xf, axis=-1, keepdims=True) + L2_EPS)


def swiglu(h, w_in, w_out):
    a, b = jnp.split(h @ w_in, 2, axis=-1)
    return (jax.nn.silu(a) * b) @ w_out


def partial_rope(x, pos):
    half = ROPE_DIM // 2
    inv_freq = jnp.exp(-(2.0 * jnp.arange(half, dtype=jnp.float32) / ROPE_DIM) * math.log(ROPE_THETA))
    ang = pos.astype(jnp.float32)[:, None] * inv_freq[None, :]
    cos = jnp.cos(ang)[:, None, None, :]
    sin = jnp.sin(ang)[:, None, None, :]
    xr = x[..., :ROPE_DIM].astype(jnp.float32)
    x1, x2 = xr[..., :half], xr[..., half:]
    rot = jnp.concatenate([x1 * cos - x2 * sin, x2 * cos + x1 * sin], axis=-1)
    return jnp.concatenate([rot.astype(x.dtype), x[..., ROPE_DIM:]], axis=-1)


def causal_conv(x, buf, w):
    L = x.shape[1]
    xp = jnp.concatenate([buf.astype(x.dtype), x], axis=1)
    acc = xp[:, 0:L] * w[0]
    for j in range(1, GDN_CONV):
        acc = acc + xp[:, j:j + L] * w[j]
    return jax.nn.silu(acc), xp[:, L:]


def gated_delta_chunked(q, k, v, g, beta, s0):
    f32 = jnp.float32
    b, L, nh, dk = q.shape
    c = min(GDN_CHUNK, L)
    pad = (-L) % c
    q, k, v, g, beta = (t.astype(f32) for t in (q, k, v, g, beta))
    if pad:
        padw = lambda t: jnp.pad(t, [(0, 0), (0, pad)] + [(0, 0)] * (t.ndim - 2))
        q, k, v, g, beta = (padw(t) for t in (q, k, v, g, beta))
    n = (L + pad) // c
    chunks = lambda t: jnp.moveaxis(t.reshape((b, n, c) + t.shape[2:]), 3, 1)
    q = chunks(q) * dk ** -0.5
    k = chunks(k)
    v = chunks(v)
    g = chunks(g)
    beta = chunks(beta)
    gc = jnp.cumsum(g, axis=-1)
    tri = jnp.tril(jnp.ones((c, c), bool))
    strict = jnp.tril(jnp.ones((c, c), bool), -1)
    gdiff = gc[..., :, None] - gc[..., None, :]
    decay = jnp.where(tri, jnp.exp(jnp.where(tri, gdiff, 0.0)), 0.0)
    kb = k * beta[..., None]
    vb = v * beta[..., None]
    lmat = jnp.where(strict, jnp.einsum('bhncd,bhnsd->bhncs', kb, k) * decay, 0.0)
    eye = jnp.eye(c, dtype=f32)
    tinv = lax.linalg.triangular_solve(lmat + eye, jnp.broadcast_to(eye, lmat.shape),
                                       left_side=True, lower=True, unit_diagonal=True)
    u = tinv @ vb
    w = tinv @ (kb * jnp.exp(gc)[..., None])
    qk = jnp.where(tri, jnp.einsum('bhncd,bhnsd->bhncs', q, k) * decay, 0.0)
    qg = q * jnp.exp(gc)[..., None]
    kg = k * jnp.exp(gc[..., -1:] - gc)[..., None]
    glast = jnp.exp(gc[..., -1])

    def step(s, xs):
        qk_i, qg_i, kg_i, u_i, w_i, gl_i = xs
        v_new = u_i - w_i @ s
        o = qg_i @ s + qk_i @ v_new
        s = s * gl_i[..., None, None] + jnp.swapaxes(kg_i, -1, -2) @ v_new
        return s, o

    xs = tuple(jnp.moveaxis(t, 2, 0) for t in (qk, qg, kg, u, w, glast))
    s_fin, o = lax.scan(step, s0.astype(f32), xs)
    o = jnp.moveaxis(jnp.moveaxis(o, 0, 2), 1, 3)
    return o.reshape(b, n * c, nh, -1)[:, :L], s_fin


def online_softmax_update(carry, s, v):
    m, l, acc = carry
    m_new = jnp.maximum(m, jnp.max(s, axis=-1))
    p = jnp.exp(s - m_new[..., None])
    corr = jnp.exp(m - m_new)
    l = l * corr + jnp.sum(p, axis=-1)
    acc = acc * corr[..., None] + jnp.einsum('bhmqk,bkhd->bhmqd', p, v)
    return m_new, l, acc


def diff_attend_causal(q, k, v, lam):
    f32 = jnp.float32
    b, L = q.shape[:2]
    qb = min(Q_BLOCK, L)
    nb = L // qb
    kf = k.astype(f32)
    vf = v.astype(f32)
    qs = jnp.moveaxis((q.astype(f32) * DIFF_D ** -0.5).reshape(b, nb, qb, DIFF_HEADS, 2, DIFF_D), 1, 0)
    kpos = jnp.arange(L)

    def one_block(args):
        qi, start = args
        s = jnp.einsum('bqhmd,bkhmd->bhmqk', qi, kf)
        qpos = start + jnp.arange(qb)
        s = jnp.where(kpos[None, :] <= qpos[:, None], s, NEG_INF)
        p = jax.nn.softmax(s, axis=-1)
        a = p[:, :, 0] - lam * p[:, :, 1]
        return jnp.einsum('bhqk,bkhd->bqhd', a, vf)

    o = lax.map(one_block, (qs, jnp.arange(nb) * qb))
    return jnp.moveaxis(o, 0, 1).reshape(b, L, DIFF_HEADS, DIFF_DV)


def diff_attend_paged(q, k, v, lam, cache_k, cache_v, layer, page_table):
    f32 = jnp.float32
    bs, t = q.shape[:2]
    page = cache_k.shape[2]
    qf = q.astype(f32) * DIFF_D ** -0.5
    init = (jnp.full((bs, DIFF_HEADS, 2, t), NEG_INF, f32),
            jnp.zeros((bs, DIFF_HEADS, 2, t), f32),
            jnp.zeros((bs, DIFF_HEADS, 2, t, DIFF_DV), f32))

    def page_step(carry, phys):
        kp = cache_k[layer, phys].astype(f32).reshape(bs, page, DIFF_HEADS, 2, DIFF_D)
        vp = cache_v[layer, phys].astype(f32)
        s = jnp.einsum('bqhmd,bkhmd->bhmqk', qf, kp)
        return online_softmax_update(carry, s, vp), None

    carry, _ = lax.scan(page_step, init, page_table.T)
    s = jnp.einsum('bqhmd,bkhmd->bhmqk', qf, k.astype(f32))
    s = jnp.where(jnp.tril(jnp.ones((t, t), bool)), s, NEG_INF)
    _, l, acc = online_softmax_update(carry, s, v.astype(f32))
    o = acc / l[..., None]
    o = o[:, :, 0] - lam * o[:, :, 1]
    return jnp.transpose(o, (0, 2, 1, 3))


def hybrid_mixer(h, pos, conv_buf, gdn_s0, attend, lam_init, params):
    w_in, w_out, conv_w, a_log, dt_bias, g_norm, lam_vec, d_norm = params
    f32 = jnp.float32
    b, L, _ = h.shape
    proj = h @ w_in
    qkv, gz, gb, ga, dq, dk, dv = jnp.split(
        proj, [SPLIT_Z, SPLIT_B, SPLIT_A, SPLIT_DQ, SPLIT_DK, SPLIT_DV], axis=-1)
    qkv_act, conv_new = causal_conv(qkv, conv_buf, conv_w)
    gq, gk, gv = jnp.split(qkv_act, [GDN_QK, 2 * GDN_QK], axis=-1)
    gq = l2norm(gq.reshape(b, L, GDN_HEADS, GDN_DK))
    gk = l2norm(gk.reshape(b, L, GDN_HEADS, GDN_DK))
    gv = gv.reshape(b, L, GDN_HEADS, GDN_DV)
    beta = jax.nn.sigmoid(gb.astype(f32))
    g = -jnp.exp(a_log.astype(f32)) * jax.nn.softplus(ga.astype(f32) + dt_bias.astype(f32))
    o_g, s_new = gated_delta_chunked(gq, gk, gv, g, beta, gdn_s0)
    o_g = rmsnorm(o_g, g_norm) * jax.nn.silu(gz.reshape(b, L, GDN_HEADS, GDN_DV).astype(f32))
    dq = partial_rope(dq.reshape(b, L, DIFF_HEADS, 2, DIFF_D), pos)
    dk = partial_rope(dk.reshape(b, L, DIFF_HEADS, 2, DIFF_D), pos)
    dv = dv.reshape(b, L, DIFF_HEADS, DIFF_DV)
    lv = lam_vec.astype(f32)
    lam = jnp.exp(jnp.sum(lv[0] * lv[1])) - jnp.exp(jnp.sum(lv[2] * lv[3])) + lam_init
    o_d = rmsnorm(attend(dq, dk, dv, lam), d_norm) * (1.0 - lam_init)
    mixed = jnp.concatenate([o_g.reshape(b, L, GDN_V), o_d.reshape(b, L, DIFF_V)], axis=-1).astype(h.dtype)
    out = mixed @ w_out
    return out, (dk.reshape(b, L, DIFF_HEADS, 2 * DIFF_D), dv, s_new, conv_new)


def sandwich(x, mod, idx, weight, g_pre, g_post, fn):
    shift, scale, gate = mod[:, 3 * idx], mod[:, 3 * idx + 1], mod[:, 3 * idx + 2]
    out, aux = fn(rmsnorm(x, g_pre) * (1.0 + scale) + shift)
    return x + weight * gate * rmsnorm(out, g_post), aux


def trunk_layer(x, c, mix_fn, layer_w):
    w_ada, b_ada, n_pre, n_post, w_ffn_in, w_ffn_out = layer_w
    mod = (jax.nn.silu(c) @ w_ada + b_ada).reshape(c.shape[0], N_MOD, 1, D_MODEL)
    ffn1 = lambda h: (swiglu(h, w_ffn_in[0], w_ffn_out[0]), None)
    ffn2 = lambda h: (swiglu(h, w_ffn_in[1], w_ffn_out[1]), None)
    x, _ = sandwich(x, mod, 0, MACARON_W, n_pre[0], n_post[0], ffn1)
    x, st = sandwich(x, mod, 1, 1.0, n_pre[1], n_post[1], mix_fn)
    x, _ = sandwich(x, mod, 2, MACARON_W, n_pre[2], n_post[2], ffn2)
    return x, st


def setup_inputs(seed: int = 0) -> dict:
    key = jax.random.key(seed)
    ks = jax.random.split(key, 23)
    f32 = jnp.float32
    n_pages = PAST_LEN // PAGE_SIZE
    n_used = DEC_BATCH * n_pages
    n_phys = n_used + max(1, n_used // 4)
    nrm = lambda k, shape, s: jax.random.normal(k, shape, f32) * s
    x_prompt = nrm(ks[0], (BATCH, SEQ, D_MODEL), 1.0)
    x_sample = nrm(ks[1], (DEC_BATCH, DEC_SEQ, D_MODEL), 1.0)
    cache_k = nrm(ks[2], (DEPTH, n_phys, PAGE_SIZE, DIFF_HEADS, 2 * DIFF_D), 1.0)
    cache_v = nrm(ks[3], (DEPTH, n_phys, PAGE_SIZE, DIFF_HEADS, DIFF_DV), 1.0)
    page_table = jax.random.permutation(ks[4], n_phys)[:n_used].reshape(DEC_BATCH, n_pages).astype(jnp.int32)
    state_gdn = nrm(ks[5], (DEPTH, DEC_BATCH, GDN_HEADS, GDN_DK, GDN_DV), GDN_DK ** -0.5)
    state_conv = nrm(ks[6], (DEPTH, DEC_BATCH, GDN_CONV - 1, GDN_CONV_DIM), 1.0)
    c_prompt = nrm(ks[7], (BATCH, D_MODEL), 1.0)
    c_sample = nrm(ks[8], (DEC_BATCH, D_MODEL), 1.0)
    w_ada = nrm(ks[9], (DEPTH, D_MODEL, N_MOD * D_MODEL), 0.5 * D_MODEL ** -0.5)
    b_ada = nrm(ks[10], (DEPTH, N_MOD * D_MODEL), 0.02)
    norm_pre = 1.0 + nrm(ks[11], (DEPTH, 3, D_MODEL), 0.02)
    norm_post = 1.0 + nrm(ks[12], (DEPTH, 3, D_MODEL), 0.02)
    w_ffn_in = nrm(ks[13], (DEPTH, 2, D_MODEL, 2 * FFN_DIM), D_MODEL ** -0.5)
    w_ffn_out = nrm(ks[14], (DEPTH, 2, FFN_DIM, D_MODEL), FFN_DIM ** -0.5)
    w_mix_in = nrm(ks[15], (DEPTH, D_MODEL, IN_DIM), D_MODEL ** -0.5)
    w_mix_out = nrm(ks[16], (DEPTH, MIX_WIDTH, D_MODEL), MIX_WIDTH ** -0.5)
    gdn_conv_w = nrm(ks[17], (DEPTH, GDN_CONV, GDN_CONV_DIM), GDN_CONV ** -0.5)
    gdn_a_log = jnp.log(jax.random.uniform(ks[18], (DEPTH, GDN_HEADS), f32, 1.0, 16.0))
    dt = jnp.exp(jax.random.uniform(ks[19], (DEPTH, GDN_HEADS), f32, math.log(1e-3), math.log(1e-1)))
    gdn_dt_bias = dt + jnp.log(-jnp.expm1(-dt))
    gdn_norm = 1.0 + nrm(ks[20], (DEPTH, GDN_DV), 0.02)
    diff_lambda = nrm(ks[21], (DEPTH, 4, DIFF_D), 0.1)
    diff_norm = 1.0 + nrm(ks[22], (DEPTH, DIFF_DV), 0.02)
    return {'x_prompt': x_prompt, 'x_sample': x_sample, 'cache_k': cache_k, 'cache_v': cache_v,
            'page_table': page_table, 'state_gdn': state_gdn, 'state_conv': state_conv,
            'c_prompt': c_prompt, 'c_sample': c_sample, 'w_ada': w_ada, 'b_ada': b_ada,
            'norm_pre': norm_pre, 'norm_post': norm_post, 'w_ffn_in': w_ffn_in, 'w_ffn_out': w_ffn_out,
            'w_mix_in': w_mix_in, 'w_mix_out': w_mix_out, 'gdn_conv_w': gdn_conv_w,
            'gdn_a_log': gdn_a_log, 'gdn_dt_bias': gdn_dt_bias, 'gdn_norm': gdn_norm,
            'diff_lambda': diff_lambda, 'diff_norm': diff_norm}


def reference(x_prompt, x_sample, cache_k, cache_v, page_table, state_gdn, state_conv,
              c_prompt, c_sample, w_ada, b_ada, norm_pre, norm_post, w_ffn_in, w_ffn_out,
              w_mix_in, w_mix_out, gdn_conv_w, gdn_a_log, gdn_dt_bias, gdn_norm,
              diff_lambda, diff_norm):
    bp, lp = x_prompt.shape[:2]
    ls = x_sample.shape[1]
    past_len = page_table.shape[1] * cache_k.shape[2]
    pos_prompt = jnp.arange(lp, dtype=jnp.int32)
    pos_sample = past_len + jnp.arange(ls, dtype=jnp.int32)
    conv0 = jnp.zeros((bp, GDN_CONV - 1, GDN_CONV_DIM), x_prompt.dtype)
    s0 = jnp.zeros((bp, GDN_HEADS, GDN_DK, GDN_DV), jnp.float32)
    yp, ys = x_prompt, x_sample
    kp_l, vp_l, ks_l, vs_l, gp_l, gs_l, cp_l, cs_l = [], [], [], [], [], [], [], []
    for l in range(DEPTH):
        lam_init = 0.8 - 0.6 * math.exp(-0.3 * l)
        mix_w = (w_mix_in[l], w_mix_out[l], gdn_conv_w[l], gdn_a_log[l], gdn_dt_bias[l],
                 gdn_norm[l], diff_lambda[l], diff_norm[l])
        layer_w = (w_ada[l], b_ada[l], norm_pre[l], norm_post[l], w_ffn_in[l], w_ffn_out[l])
        mix_prompt = functools.partial(hybrid_mixer, pos=pos_prompt, conv_buf=conv0, gdn_s0=s0,
                                       attend=diff_attend_causal, lam_init=lam_init, params=mix_w)
        attend_s = functools.partial(diff_attend_paged, cache_k=cache_k, cache_v=cache_v,
                                     layer=l, page_table=page_table)
        mix_sample = functools.partial(hybrid_mixer, pos=pos_sample, conv_buf=state_conv[l],
                                       gdn_s0=state_gdn[l], attend=attend_s, lam_init=lam_init,
                                       params=mix_w)
        yp, (kp, vp, gp, cp) = trunk_layer(yp, c_prompt, mix_prompt, layer_w)
        ys, (ksm, vsm, gsm, csm) = trunk_layer(ys, c_sample, mix_sample, layer_w)
        kp_l.append(kp); vp_l.append(vp); gp_l.append(gp); cp_l.append(cp)
        ks_l.append(ksm); vs_l.append(vsm); gs_l.append(gsm); cs_l.append(csm)
    y_prompt = yp
    y_sample = ys
    k_prompt = jnp.stack(kp_l)
    v_prompt = jnp.stack(vp_l)
    k_sample = jnp.stack(ks_l)
    v_sample = jnp.stack(vs_l)
    gdn_prompt = jnp.stack(gp_l)
    gdn_sample = jnp.stack(gs_l)
    conv_prompt = jnp.stack(cp_l)
    conv_sample = jnp.stack(cs_l)
    return (y_prompt, y_sample, k_prompt, v_prompt, k_sample, v_sample, gdn_prompt, gdn_sample, conv_prompt, conv_sample)
```

```python
import functools
import math

import jax
import jax.numpy as jnp
from jax import lax
from jax.experimental import pallas as pl
from jax.experimental.pallas import tpu as pltpu

F32 = jnp.float32
BF16 = jnp.bfloat16
HIGHEST = lax.Precision.HIGHEST

D_MODEL = 1024
GDN_HEADS = 4
GDN_DK = 128
GDN_DV = 128
GDN_CONV = 4
GDN_QK = GDN_HEADS * GDN_DK
GDN_V = GDN_HEADS * GDN_DV
GDN_CONV_DIM = 2 * GDN_QK + GDN_V
DIFF_HEADS = 4
DIFF_D = 64
DIFF_DV = 128
DIFF_QK = DIFF_HEADS * 2 * DIFF_D
DIFF_V = DIFF_HEADS * DIFF_DV
ROPE_DIM = DIFF_D // 4
ROPE_THETA = 500000.0
SPLIT_Z = GDN_CONV_DIM
SPLIT_B = SPLIT_Z + GDN_V
SPLIT_A = SPLIT_B + GDN_HEADS
SPLIT_DQ = SPLIT_A + GDN_HEADS
SPLIT_DK = SPLIT_DQ + DIFF_QK
SPLIT_DV = SPLIT_DK + DIFF_QK
FFN_DIM = 2816
MACARON_W = 0.5
N_MOD = 9
NORM_EPS = 1e-6
L2_EPS = 1e-6
NEG_INF = -1e30

LANES = 128
SUBLANES = 8
VMEM_LIMIT_BYTES = 56 * 1024 * 1024

COL_QKV = 0
COL_Z = GDN_CONV_DIM
COL_DQ = COL_Z + GDN_V
COL_DK = COL_DQ + DIFF_QK
COL_DV = COL_DK + DIFF_QK
COL_BA = COL_DV + DIFF_V
MIXIN_COLS = COL_BA + LANES

ROW_TILE = 512
FFN_CHUNK = 256
ATT_TILE = 512
GDN_TILE = 128
SAMPLE_PAD = 8
SAMPLE_SEQ_BLOCK = GDN_TILE // SAMPLE_PAD
PAGES_PER_STEP = 16


def _params(sem):
    return pltpu.CompilerParams(dimension_semantics=sem, vmem_limit_bytes=VMEM_LIMIT_BYTES)


def _rms(x, g):
    return x * lax.rsqrt(jnp.mean(x * x, axis=-1, keepdims=True) + NORM_EPS) * g


def _silu(x):
    return x * jax.nn.sigmoid(x)


def _dot(a, b, precision=None):
    return jnp.dot(a, b, preferred_element_type=F32, precision=precision)


def _dot_nt(a, b, precision=None):
    return lax.dot_general(a, b, (((1,), (1,)), ((), ())), preferred_element_type=F32,
                           precision=precision)


def _ada_kernel(c_ref, w_ref, b_ref, o_ref):
    o_ref[...] = _dot(_silu(c_ref[...]), w_ref[...], HIGHEST) + b_ref[...]


def _ada_mod(c, w_ada, b_ada):
    rows = c.shape[0]
    n = w_ada.shape[1] // D_MODEL
    return pl.pallas_call(
        _ada_kernel,
        grid=(n,),
        in_specs=[pl.BlockSpec((rows, D_MODEL), lambda j: (0, 0)),
                  pl.BlockSpec((D_MODEL, D_MODEL), lambda j: (0, j)),
                  pl.BlockSpec((1, D_MODEL), lambda j: (0, j))],
        out_specs=pl.BlockSpec((rows, D_MODEL), lambda j: (0, j)),
        out_shape=jax.ShapeDtypeStruct((rows, w_ada.shape[1]), F32),
        compiler_params=_params(("arbitrary",)),
        name="ada_mod",
    )(c, w_ada, b_ada.reshape(1, -1))


def _ffn_kernel(x_ref, sh_ref, sc_ref, gt_ref, gpre_ref, gpost_ref, win_ref, wout_ref, y_ref,
                h_ref, acc_ref):
    x = x_ref[...]
    h_ref[...] = (_rms(x, gpre_ref[...]) * (1.0 + sc_ref[0]) + sh_ref[0]).astype(BF16)
    for c in range(FFN_DIM // FFN_CHUNK):
        lo = c * FFN_CHUNK
        h = h_ref[...]
        a = _dot(h, win_ref[:, lo:lo + FFN_CHUNK])
        b = _dot(h, win_ref[:, FFN_DIM + lo:FFN_DIM + lo + FFN_CHUNK])
        contrib = _dot((_silu(a) * b).astype(BF16), wout_ref[lo:lo + FFN_CHUNK, :])
        if c == 0:
            acc_ref[...] = contrib
        else:
            acc_ref[...] += contrib
    y_ref[...] = x + MACARON_W * gt_ref[0] * _rms(acc_ref[...], gpost_ref[...])


def _mod_spec(mod_rows_per_group, tm, tiles_per_group):
    if mod_rows_per_group == 1:
        return pl.BlockSpec((1, 1, D_MODEL), lambda i: (i // tiles_per_group, 0, 0))
    return pl.BlockSpec((1, tm, D_MODEL), lambda i: (0, i, 0))


def _ffn_sublayer(x, shift, scale, gate, g_pre, g_post, w_in, w_out, tiles_per_group):
    rows = x.shape[0]
    tm = min(ROW_TILE, rows)
    mspec = _mod_spec(shift.shape[1], tm, tiles_per_group)
    vec = pl.BlockSpec((1, D_MODEL), lambda i: (0, 0))
    resident = dict(pipeline_mode=pl.Buffered(1))
    return pl.pallas_call(
        _ffn_kernel,
        grid=(rows // tm,),
        in_specs=[pl.BlockSpec((tm, D_MODEL), lambda i: (i, 0)), mspec, mspec, mspec, vec, vec,
                  pl.BlockSpec((D_MODEL, 2 * FFN_DIM), lambda i: (0, 0), **resident),
                  pl.BlockSpec((FFN_DIM, D_MODEL), lambda i: (0, 0), **resident)],
        out_specs=pl.BlockSpec((tm, D_MODEL), lambda i: (i, 0)),
        out_shape=jax.ShapeDtypeStruct((rows, D_MODEL), F32),
        scratch_shapes=[pltpu.VMEM((tm, D_MODEL), BF16), pltpu.VMEM((tm, D_MODEL), F32)],
        compiler_params=_params(("arbitrary",)),
        name="ffn_sublayer",
    )(x, shift, scale, gate, g_pre.reshape(1, -1), g_post.reshape(1, -1), w_in, w_out)


def _mixin_kernel(x_ref, sh_ref, sc_ref, gpre_ref, w_ref, cos_ref, sina_ref, sinb_ref,
                  qkv_ref, z_ref, ba_ref, qbf_ref, k_ref, kbf_ref, v_ref, vbf_ref, h_ref):
    x = x_ref[...]
    h_ref[...] = (_rms(x, gpre_ref[...]) * (1.0 + sc_ref[0]) + sh_ref[0]).astype(BF16)

    def proj(lo, width):
        return _dot(h_ref[...], w_ref[:, lo:lo + width])

    for c in range(GDN_CONV_DIM // DIFF_QK):
        qkv_ref[:, c * DIFF_QK:(c + 1) * DIFF_QK] = proj(COL_QKV + c * DIFF_QK, DIFF_QK)
    z_ref[...] = proj(COL_Z, GDN_V)
    ba_ref[...] = proj(COL_BA, LANES)

    cos, sina, sinb = cos_ref[...], sina_ref[...], sinb_ref[...]
    half = ROPE_DIM // 2

    def rope(xh):
        return (xh * cos + pltpu.roll(xh, LANES - half, 1) * sina + pltpu.roll(xh, half, 1) * sinb)

    q = proj(COL_DQ, DIFF_QK)
    for hh in range(DIFF_HEADS):
        sl = slice(hh * LANES, (hh + 1) * LANES)
        qbf_ref[:, sl] = (rope(q[:, sl]) * (DIFF_D ** -0.5)).astype(BF16)
    k = proj(COL_DK, DIFF_QK)
    for hh in range(DIFF_HEADS):
        sl = slice(hh * LANES, (hh + 1) * LANES)
        kr = rope(k[:, sl])
        k_ref[:, sl] = kr
        kbf_ref[:, sl] = kr.astype(BF16)
    v = proj(COL_DV, DIFF_V)
    v_ref[...] = v
    vbf_ref[...] = v.astype(BF16)


def _mix_in(x, shift, scale, g_pre, w_re, rope_tabs, tiles_per_group):
    rows = x.shape[0]
    tm = min(ROW_TILE, rows)
    mspec = _mod_spec(shift.shape[1], tm, tiles_per_group)
    tab_tiles = rope_tabs[0].shape[0] // tm
    tspec = pl.BlockSpec((tm, LANES), lambda i: (i % tab_tiles, 0))

    def out(width):
        return pl.BlockSpec((tm, width), lambda i: (i, 0))

    widths = [(GDN_CONV_DIM, F32), (GDN_V, F32), (LANES, F32), (DIFF_QK, BF16), (DIFF_QK, F32),
              (DIFF_QK, BF16), (DIFF_V, F32), (DIFF_V, BF16)]
    return pl.pallas_call(
        _mixin_kernel,
        grid=(rows // tm,),
        in_specs=[pl.BlockSpec((tm, D_MODEL), lambda i: (i, 0)), mspec, mspec,
                  pl.BlockSpec((1, D_MODEL), lambda i: (0, 0)),
                  pl.BlockSpec((D_MODEL, MIXIN_COLS), lambda i: (0, 0), pipeline_mode=pl.Buffered(1)),
                  tspec, tspec, tspec],
        out_specs=[out(w) for w, _ in widths],
        out_shape=[jax.ShapeDtypeStruct((rows, w), dt) for w, dt in widths],
        scratch_shapes=[pltpu.VMEM((tm, D_MODEL), BF16)],
        compiler_params=_params(("arbitrary",)),
        name="mix_in",
    )(x, shift, scale, g_pre.reshape(1, -1), w_re, *rope_tabs)


def _rope_tables(pos):
    half = ROPE_DIM // 2
    inv_freq = jnp.exp(-(2.0 * jnp.arange(half, dtype=F32) / ROPE_DIM) * math.log(ROPE_THETA))
    ang = pos.astype(F32)[:, None] * inv_freq[None, :]
    cos, sin = jnp.cos(ang), jnp.sin(ang)
    n = pos.shape[0]
    pad = jnp.zeros((n, DIFF_D - ROPE_DIM), F32)
    zero = jnp.zeros((n, half), F32)
    cos_map = jnp.concatenate([cos, cos, pad + 1.0], axis=1)
    sina_map = jnp.concatenate([-sin, zero, pad], axis=1)
    sinb_map = jnp.concatenate([zero, sin, pad], axis=1)
    return tuple(jnp.concatenate([t, t], axis=1) for t in (cos_map, sina_map, sinb_map))


def _lambda(lam_ref, lam_init):
    lv = lam_ref[...]
    s1 = jnp.sum(lv[0:1] * lv[1:2], axis=-1, keepdims=True)
    s2 = jnp.sum(lv[2:3] * lv[3:4], axis=-1, keepdims=True)
    return jnp.exp(s1) - jnp.exp(s2) + lam_init


def _softmax_update(m_ref, l_ref, acc_ref, idx, s, pv_fn):
    m_old = m_ref[idx]
    m_new = jnp.maximum(m_old, jnp.max(s, axis=-1, keepdims=True))
    p = jnp.exp(s - m_new)
    corr = jnp.exp(m_old - m_new)
    l_ref[idx] = l_ref[idx] * corr + jnp.sum(p, axis=-1, keepdims=True)
    acc_ref[idx] = acc_ref[idx] * corr + pv_fn(p.astype(BF16))
    m_ref[idx] = m_new


def _init_softmax(m_ref, l_ref, acc_ref):
    m_ref[...] = jnp.full(m_ref.shape, NEG_INF, F32)
    l_ref[...] = jnp.zeros(l_ref.shape, F32)
    acc_ref[...] = jnp.zeros(acc_ref.shape, F32)


def _flash_kernel(q_ref, k_ref, v_ref, lam_ref, dn_ref, o_ref, m_ref, l_ref, acc_ref, *, lam_init):
    iq, ik = pl.program_id(2), pl.program_id(3)

    @pl.when(ik == 0)
    def _():
        _init_softmax(m_ref, l_ref, acc_ref)

    def update(masked):
        q, k, v = q_ref[...], k_ref[...], v_ref[...]
        first_map = lax.broadcasted_iota(jnp.int32, q.shape, 1) < DIFF_D
        for m in range(2):
            s = _dot_nt(jnp.where(first_map == (m == 0), q, jnp.zeros_like(q)), k)
            if masked:
                rows = lax.broadcasted_iota(jnp.int32, s.shape, 0)
                cols = lax.broadcasted_iota(jnp.int32, s.shape, 1)
                s = jnp.where(cols <= rows, s, NEG_INF)
            _softmax_update(m_ref, l_ref, acc_ref, m, s, lambda p: _dot(p, v))

    @pl.when(ik < iq)
    def _():
        update(False)

    @pl.when(ik == iq)
    def _():
        update(True)
        lam = _lambda(lam_ref, lam_init)
        o = acc_ref[0] / l_ref[0] - lam * (acc_ref[1] / l_ref[1])
        o_ref[...] = (_rms(o, dn_ref[...]) * (1.0 - lam_init)).astype(o_ref.dtype)


def _prompt_attention(qbf, kbf, vbf, lam_vec, d_norm, batch, seq, lam_init):
    tq = min(ATT_TILE, seq)
    nq = seq // tq
    kv_map = lambda b, h, iq, ik: (b * nq + jnp.minimum(ik, iq), h)
    return pl.pallas_call(
        functools.partial(_flash_kernel, lam_init=lam_init),
        grid=(batch, DIFF_HEADS, nq, nq),
        in_specs=[pl.BlockSpec((tq, LANES), lambda b, h, iq, ik: (b * nq + iq, h)),
                  pl.BlockSpec((tq, LANES), kv_map),
                  pl.BlockSpec((tq, LANES), kv_map),
                  pl.BlockSpec((4, DIFF_D), lambda b, h, iq, ik: (0, 0)),
                  pl.BlockSpec((1, DIFF_DV), lambda b, h, iq, ik: (0, 0))],
        out_specs=pl.BlockSpec((tq, LANES), lambda b, h, iq, ik: (b * nq + iq, h)),
        out_shape=jax.ShapeDtypeStruct((batch * seq, DIFF_V), BF16),
        scratch_shapes=[pltpu.VMEM((2, tq, 1), F32), pltpu.VMEM((2, tq, 1), F32),
                        pltpu.VMEM((2, tq, DIFF_DV), F32)],
        compiler_params=_params(("arbitrary",) * 4),
        name="prompt_diff_attention",
    )(qbf, kbf, vbf, lam_vec, d_norm.reshape(1, -1))


def _paged_kernel(pt_ref, q_ref, knew_ref, vnew_ref, lam_ref, dn_ref, *rest, lam_init, n_tok):
    del pt_ref
    n = PAGES_PER_STEP
    k_refs, v_refs = rest[:n], rest[n:2 * n]
    o_ref, m_ref, l_ref, acc_ref = rest[2 * n:]
    j = pl.program_id(1)

    @pl.when(j == 0)
    def _():
        _init_softmax(m_ref, l_ref, acc_ref)

    for hh in range(DIFF_HEADS):
        sl = slice(hh * LANES, (hh + 1) * LANES)
        qh = q_ref[hh]
        s = jnp.concatenate([_dot_nt(qh, k_refs[i][:, sl].astype(BF16)) for i in range(n)], axis=1)

        def pv(p, sl=sl):
            out = _dot(p[:, 0:LANES], v_refs[0][:, sl].astype(BF16))
            for i in range(1, n):
                out = out + _dot(p[:, i * LANES:(i + 1) * LANES], v_refs[i][:, sl].astype(BF16))
            return out

        _softmax_update(m_ref, l_ref, acc_ref, hh, s, pv)

    @pl.when(j == pl.num_programs(1) - 1)
    def _():
        lam = _lambda(lam_ref, lam_init)
        zeros = jnp.zeros((LANES - SAMPLE_PAD, LANES), BF16)
        for hh in range(DIFF_HEADS):
            sl = slice(hh * LANES, (hh + 1) * LANES)
            qh = q_ref[hh]
            knew = jnp.concatenate([knew_ref[:, sl].astype(BF16), zeros], axis=0)
            vnew = jnp.concatenate([vnew_ref[:, sl].astype(BF16), zeros], axis=0)
            s = _dot_nt(qh, knew)
            tok = lax.broadcasted_iota(jnp.int32, s.shape, 0) & (n_tok - 1)
            key = lax.broadcasted_iota(jnp.int32, s.shape, 1)
            s = jnp.where(key <= tok, s, NEG_INF)
            _softmax_update(m_ref, l_ref, acc_ref, hh, s, lambda p, vnew=vnew: _dot(p, vnew))
            o = acc_ref[hh] / l_ref[hh]
            od = o - lam * pltpu.roll(o, SAMPLE_PAD - n_tok, 0)
            o_ref[:, sl] = _rms(od, dn_ref[...]) * (1.0 - lam_init)


def _paged_attention(q_rows, k_new, v_new, cache_k, cache_v, page_table, lam_vec, d_norm, lam_init, n_tok):
    n_seq, n_pages = page_table.shape
    n = PAGES_PER_STEP
    page = cache_k.shape[1]
    steps = n_pages // n
    seq_spec = lambda shape: pl.BlockSpec((None,) + shape, lambda b, j, pt: (b,) + (0,) * len(shape))
    page_spec = lambda i: pl.BlockSpec((None, page, DIFF_V), lambda b, j, pt, i=i: (pt[b, j * n + i], 0, 0))
    grid_spec = pltpu.PrefetchScalarGridSpec(
        num_scalar_prefetch=1,
        grid=(n_seq, steps),
        in_specs=[seq_spec((DIFF_HEADS, SAMPLE_PAD, LANES)), seq_spec((SAMPLE_PAD, DIFF_QK)),
                  seq_spec((SAMPLE_PAD, DIFF_V)),
                  pl.BlockSpec((4, DIFF_D), lambda b, j, pt: (0, 0)),
                  pl.BlockSpec((1, DIFF_DV), lambda b, j, pt: (0, 0))]
                 + [page_spec(i) for i in range(n)] * 2,
        out_specs=seq_spec((SAMPLE_PAD, DIFF_V)),
        scratch_shapes=[pltpu.VMEM((DIFF_HEADS, SAMPLE_PAD, 1), F32),
                        pltpu.VMEM((DIFF_HEADS, SAMPLE_PAD, 1), F32),
                        pltpu.VMEM((DIFF_HEADS, SAMPLE_PAD, DIFF_DV), F32)])
    return pl.pallas_call(
        functools.partial(_paged_kernel, lam_init=lam_init, n_tok=n_tok),
        grid_spec=grid_spec,
        out_shape=jax.ShapeDtypeStruct((n_seq, SAMPLE_PAD, DIFF_V), F32),
        compiler_params=_params(("arbitrary", "arbitrary")),
        name="paged_diff_attention",
    )(page_table, q_rows, k_new, v_new, lam_vec, d_norm.reshape(1, -1),
      *([cache_k] * n), *([cache_v] * n))


def _gates(ba, alog_row, dtb_row):
    beta = jax.nn.sigmoid(ba)
    x = ba + dtb_row
    softplus = jnp.maximum(x, 0.0) + jnp.log1p(jnp.exp(-jnp.abs(x)))
    return beta, -jnp.exp(alog_row) * softplus


def _chunk_masks(rows, chunk):
    ri = lax.broadcasted_iota(jnp.int32, (rows, rows), 0)
    ci = lax.broadcasted_iota(jnp.int32, (rows, rows), 1)
    shift = chunk.bit_length() - 1
    same = lax.shift_right_logical(ri, shift) == lax.shift_right_logical(ci, shift)
    return same, same & (ci <= ri), same & (ci < ri), (ri == ci).astype(F32)


def _unit_lower_inverse(lmat, eye, chunk):
    n = -lmat
    p = eye + n
    m = n
    span = 2
    while span < chunk:
        m = _dot(m, m, HIGHEST)
        p = p + _dot(p, m, HIGHEST)
        span *= 2
    return p


def _gdn_intra(q, k, v, beta, gc_col, gc_row, gtot_col, tri, strict, eye, chunk):
    decay = jnp.where(tri, jnp.exp(jnp.where(tri, gc_col - gc_row, 0.0)), 0.0)
    kb = k * beta
    lmat = jnp.where(strict, _dot_nt(kb, k, HIGHEST) * decay, 0.0)
    tinv = _unit_lower_inverse(lmat, eye, chunk)
    eg = jnp.exp(gc_col)
    u = _dot(tinv, v * beta, HIGHEST)
    w = _dot(tinv, kb * eg, HIGHEST)
    qs = q * (GDN_DK ** -0.5)
    qk = _dot_nt(qs, k, HIGHEST) * decay
    return u, w, qk, qs * eg, k * jnp.exp(gtot_col - gc_col)


def _l2norm(x):
    return x * lax.rsqrt(jnp.sum(x * x, axis=-1, keepdims=True) + L2_EPS)


def _conv_silu(xp_ref, w_ref, rows, first):
    acc = xp_ref[pl.ds(first, rows), :] * w_ref[0:1, :]
    for j in range(1, GDN_CONV):
        acc = acc + xp_ref[pl.ds(first + j, rows), :] * w_ref[j:j + 1, :]
    return _silu(acc)


def _head_qkv(act, hh):
    q = _l2norm(act[:, hh * GDN_DK:(hh + 1) * GDN_DK])
    k = _l2norm(act[:, GDN_QK + hh * GDN_DK:GDN_QK + (hh + 1) * GDN_DK])
    v = act[:, 2 * GDN_QK + hh * GDN_DV:2 * GDN_QK + (hh + 1) * GDN_DV]
    return q, k, v


def _gate_rows(a_log, dt_bias):
    pad = lambda p: jnp.pad(p.astype(F32), (GDN_HEADS, LANES - 2 * GDN_HEADS)).reshape(1, LANES)
    return pad(a_log), pad(dt_bias)


def _gdn_prompt_kernel(qkv_ref, ba_ref, z_ref, cw_ref, alog_ref, dtb_ref, gn_ref, og_ref, sfin_ref,
                       s_ref, xp_ref):
    i = pl.program_id(1)
    t = GDN_TILE
    halo = SUBLANES

    @pl.when(i == 0)
    def _():
        s_ref[...] = jnp.zeros(s_ref.shape, F32)
        xp_ref[0:halo, :] = jnp.zeros((halo, GDN_CONV_DIM), F32)

    xp_ref[halo:halo + t, :] = qkv_ref[...]
    act = _conv_silu(xp_ref, cw_ref, t, halo - (GDN_CONV - 1))
    xp_ref[0:halo, :] = qkv_ref[t - halo:t, :]

    beta, g = _gates(ba_ref[...], alog_ref[...], dtb_ref[...])
    _, tri, strict, eye = _chunk_masks(t, t)
    gc = _dot(tri.astype(F32), g, HIGHEST)
    gc_t = gc.T
    gn = gn_ref[...]

    for hh in range(GDN_HEADS):
        q, k, v = _head_qkv(act, hh)
        gl = GDN_HEADS + hh
        gtot = gc[t - 1:t, gl:gl + 1]
        u, w, qk, qg, kg = _gdn_intra(q, k, v, beta[:, hh:hh + 1], gc[:, gl:gl + 1],
                                      gc_t[gl:gl + 1, :], gtot, tri, strict, eye, t)
        state = s_ref[hh]
        v_new = u - _dot(w, state, HIGHEST)
        o = _dot(qg, state, HIGHEST) + _dot(qk, v_new, HIGHEST)
        s_ref[hh] = state * jnp.exp(gtot) + _dot(kg.T, v_new, HIGHEST)
        sl = slice(hh * GDN_DV, (hh + 1) * GDN_DV)
        og_ref[:, sl] = (_rms(o, gn) * _silu(z_ref[:, sl])).astype(og_ref.dtype)

    @pl.when(i == pl.num_programs(1) - 1)
    def _():
        sfin_ref[...] = s_ref[...]


def _gdn_prompt(qkv, ba, z, conv_w, a_log, dt_bias, g_norm, batch, seq):
    t = GDN_TILE
    nt = seq // t
    alog_row, dtb_row = _gate_rows(a_log, dt_bias)
    row = lambda width: pl.BlockSpec((t, width), lambda b, i: (b * nt + i, 0))
    const = lambda shape: pl.BlockSpec(shape, lambda b, i: (0,) * len(shape))
    return pl.pallas_call(
        _gdn_prompt_kernel,
        grid=(batch, nt),
        in_specs=[row(GDN_CONV_DIM), row(LANES), row(GDN_V), const((GDN_CONV, GDN_CONV_DIM)),
                  const((1, LANES)), const((1, LANES)), const((1, GDN_DV))],
        out_specs=[row(GDN_V),
                   pl.BlockSpec((None, GDN_HEADS, GDN_DK, GDN_DV), lambda b, i: (b, 0, 0, 0))],
        out_shape=[jax.ShapeDtypeStruct((batch * seq, GDN_V), BF16),
                   jax.ShapeDtypeStruct((batch, GDN_HEADS, GDN_DK, GDN_DV), F32)],
        scratch_shapes=[pltpu.VMEM((GDN_HEADS, GDN_DK, GDN_DV), F32),
                        pltpu.VMEM((t + SUBLANES, GDN_CONV_DIM), F32)],
        compiler_params=_params(("arbitrary", "arbitrary")),
        name="gdn_prompt",
    )(qkv, ba, z, conv_w, alog_row, dtb_row, g_norm.reshape(1, -1))


def _gdn_sample_kernel(xp_in_ref, ba_ref, z_ref, s0_ref, cw_ref, alog_ref, dtb_ref, gn_ref,
                       og_ref, snew_ref, xp_ref, *, n_tok):
    rows = GDN_TILE
    c = SAMPLE_PAD
    xp_ref[0:rows, :] = xp_in_ref[...]
    xp_ref[rows:rows + SUBLANES, :] = jnp.zeros((SUBLANES, GDN_CONV_DIM), F32)
    act = _conv_silu(xp_ref, cw_ref, rows, 0)
    valid = (lax.broadcasted_iota(jnp.int32, (rows, 1), 0) & (c - 1)) < n_tok
    beta, g = _gates(ba_ref[...], alog_ref[...], dtb_ref[...])
    beta = jnp.where(valid, beta, 0.0)
    g = jnp.where(valid, g, 0.0)
    same, tri, strict, eye = _chunk_masks(rows, c)
    gc = _dot(tri.astype(F32), g, HIGHEST)
    gtot = _dot(same.astype(F32), g, HIGHEST)
    gc_t = gc.T
    gn = gn_ref[...]
    lane_seq = lax.shift_right_logical(lax.broadcasted_iota(jnp.int32, (rows, rows), 1),
                                       c.bit_length() - 1)

    for hh in range(GDN_HEADS):
        q, k, v = _head_qkv(act, hh)
        q = jnp.where(valid, q, 0.0)
        k = jnp.where(valid, k, 0.0)
        v = jnp.where(valid, v, 0.0)
        gl = GDN_HEADS + hh
        gtot_col = gtot[:, gl:gl + 1]
        u, w, qk, qg, kg = _gdn_intra(q, k, v, beta[:, hh:hh + 1], gc[:, gl:gl + 1],
                                      gc_t[gl:gl + 1, :], gtot_col, tri, strict, eye, c)
        states = [s0_ref[s, hh] for s in range(SAMPLE_SEQ_BLOCK)]
        v_new, o_state = [], []
        for s, state in enumerate(states):
            r = slice(s * c, (s + 1) * c)
            v_new.append(u[r] - _dot(w[r], state, HIGHEST))
            o_state.append(_dot(qg[r], state, HIGHEST))
        v_new = jnp.concatenate(v_new, axis=0)
        o = jnp.concatenate(o_state, axis=0) + _dot(qk, v_new, HIGHEST)
        kg_t = kg.T
        decay = jnp.exp(gtot_col)
        for s, state in enumerate(states):
            upd = _dot(jnp.where(lane_seq == s, kg_t, 0.0), v_new, HIGHEST)
            snew_ref[s, hh] = state * decay[s * c:s * c + 1, :] + upd
        sl = slice(hh * GDN_DV, (hh + 1) * GDN_DV)
        og_ref[:, sl] = (_rms(o, gn) * _silu(z_ref[:, sl])).astype(og_ref.dtype)


def _gdn_sample(xp, ba, z, state, conv_w, a_log, dt_bias, g_norm, n_tok):
    n_seq = state.shape[0]
    sb = SAMPLE_SEQ_BLOCK
    rows = GDN_TILE
    alog_row, dtb_row = _gate_rows(a_log, dt_bias)
    row = lambda width: pl.BlockSpec((rows, width), lambda i: (i, 0))
    const = lambda shape: pl.BlockSpec(shape, lambda i: (0,) * len(shape))
    sspec = pl.BlockSpec((sb, GDN_HEADS, GDN_DK, GDN_DV), lambda i: (i, 0, 0, 0))
    return pl.pallas_call(
        functools.partial(_gdn_sample_kernel, n_tok=n_tok),
        grid=(n_seq // sb,),
        in_specs=[row(GDN_CONV_DIM), row(LANES), row(GDN_V), sspec, const((GDN_CONV, GDN_CONV_DIM)),
                  const((1, LANES)), const((1, LANES)), const((1, GDN_DV))],
        out_specs=[row(GDN_V), sspec],
        out_shape=[jax.ShapeDtypeStruct((n_seq * SAMPLE_PAD, GDN_V), BF16),
                   jax.ShapeDtypeStruct(state.shape, F32)],
        scratch_shapes=[pltpu.VMEM((rows + SUBLANES, GDN_CONV_DIM), F32)],
        compiler_params=_params(("arbitrary",)),
        name="gdn_sample",
    )(xp, ba, z, state, conv_w, alog_row, dtb_row, g_norm.reshape(1, -1))


def _mixout_kernel(x_ref, og_ref, od_ref, gt_ref, gpost_ref, w_ref, y_ref):
    out = (_dot(og_ref[...].astype(BF16), w_ref[0:GDN_V, :])
           + _dot(od_ref[...].astype(BF16), w_ref[GDN_V:GDN_V + DIFF_V, :]))
    y_ref[...] = x_ref[...] + gt_ref[0] * _rms(out, gpost_ref[...])


def _mix_out(x, og, od, gate, g_post, w_out, tiles_per_group):
    rows = x.shape[0]
    tm = min(ROW_TILE, rows)
    mspec = _mod_spec(gate.shape[1], tm, tiles_per_group)
    row = lambda width: pl.BlockSpec((tm, width), lambda i: (i, 0))
    return pl.pallas_call(
        _mixout_kernel,
        grid=(rows // tm,),
        in_specs=[row(D_MODEL), row(GDN_V), row(DIFF_V), mspec,
                  pl.BlockSpec((1, D_MODEL), lambda i: (0, 0)),
                  pl.BlockSpec((GDN_V + DIFF_V, D_MODEL), lambda i: (0, 0), pipeline_mode=pl.Buffered(1))],
        out_specs=row(D_MODEL),
        out_shape=jax.ShapeDtypeStruct((rows, D_MODEL), F32),
        compiler_params=_params(("arbitrary",)),
        name="mix_out",
    )(x, og, od, gate, g_post.reshape(1, -1), w_out)


def _reorder_mix_in(w):
    gates = jnp.pad(w[:, SPLIT_Z + GDN_V:SPLIT_DQ], ((0, 0), (0, LANES - 2 * GDN_HEADS)))
    return jnp.concatenate([w[:, :SPLIT_Z + GDN_V], w[:, SPLIT_DQ:], gates], axis=1).astype(BF16)


def _sample_q_rows(qbf, n_seq, n_tok):
    q = qbf.reshape(n_seq, n_tok, DIFF_HEADS, 2, DIFF_D).transpose(0, 2, 3, 1, 4)
    q = q[:, :, :, :, None, :] * jnp.eye(2, dtype=qbf.dtype)[None, None, :, None, :, None]
    q = q.reshape(n_seq, DIFF_HEADS, 2, n_tok, 2 * DIFF_D)
    q = jnp.pad(q, ((0, 0), (0, 0), (0, 0), (0, SAMPLE_PAD // 2 - n_tok), (0, 0)))
    return q.reshape(n_seq, DIFF_HEADS, SAMPLE_PAD, 2 * DIFF_D)


def _pad_tokens(x, n_seq, n_tok):
    x = x.reshape(n_seq, n_tok, x.shape[-1])
    return jnp.pad(x, ((0, 0), (0, SAMPLE_PAD - n_tok), (0, 0)))


def kernel(x_prompt, x_sample, cache_k, cache_v, page_table, state_gdn, state_conv, c_prompt, c_sample,
           w_ada, b_ada, norm_pre, norm_post, w_ffn_in, w_ffn_out, w_mix_in, w_mix_out, gdn_conv_w,
           gdn_a_log, gdn_dt_bias, gdn_norm, diff_lambda, diff_norm):
    bp, lp, _ = x_prompt.shape
    bs, ls, _ = x_sample.shape
    depth = w_ada.shape[0]
    assert depth == 1 and 2 * ls == SAMPLE_PAD and bs % SAMPLE_SEQ_BLOCK == 0
    assert page_table.shape[1] % PAGES_PER_STEP == 0 and lp % ROW_TILE == 0
    layer = 0
    lam_init = 0.8 - 0.6 * math.exp(-0.3 * layer)
    past_len = page_table.shape[1] * cache_k.shape[2]
    rows_p, rows_s = bp * lp, bs * ls
    tiles_p = lp // min(ROW_TILE, rows_p)

    c_all = jnp.concatenate([c_prompt, c_sample], axis=0)
    c_all = jnp.pad(c_all, ((0, (-c_all.shape[0]) % SUBLANES), (0, 0)))
    mod = _ada_mod(c_all, w_ada[layer], b_ada[layer])
    mod_p = mod[:bp].reshape(bp, N_MOD, 1, D_MODEL)
    mod_s = mod[bp:bp + bs].reshape(bs, N_MOD, D_MODEL)
    mods_p = [mod_p[:, i] for i in range(N_MOD)]
    mods_s = [jnp.repeat(mod_s[:, i], ls, axis=0)[None] for i in range(N_MOD)]

    w_in_bf = w_ffn_in[layer].astype(BF16)
    w_out_bf = w_ffn_out[layer].astype(BF16)
    w_mix_in_bf = _reorder_mix_in(w_mix_in[layer])
    w_mix_out_bf = w_mix_out[layer].astype(BF16)
    n_pre, n_post = norm_pre[layer], norm_post[layer]
    conv_w, a_log, dt_bias = gdn_conv_w[layer], gdn_a_log[layer], gdn_dt_bias[layer]
    g_norm, lam_vec, d_norm = gdn_norm[layer], diff_lambda[layer], diff_norm[layer]

    def ffn(x, mods, idx, tiles):
        return _ffn_sublayer(x, mods[3 * idx], mods[3 * idx + 1], mods[3 * idx + 2], n_pre[idx], n_post[idx],
                             w_in_bf[idx // 2], w_out_bf[idx // 2], tiles)

    xp = ffn(x_prompt.reshape(rows_p, D_MODEL), mods_p, 0, tiles_p)
    tabs_p = _rope_tables(jnp.arange(lp, dtype=jnp.int32))
    qkv_p, z_p, ba_p, qbf_p, k_p, kbf_p, v_p, vbf_p = _mix_in(xp, mods_p[3], mods_p[4], n_pre[1], w_mix_in_bf,
                                                              tabs_p, tiles_p)
    od_p = _prompt_attention(qbf_p, kbf_p, vbf_p, lam_vec, d_norm, bp, lp, lam_init)
    og_p, gdn_p = _gdn_prompt(qkv_p, ba_p, z_p, conv_w, a_log, dt_bias, g_norm, bp, lp)
    xp = _mix_out(xp, og_p, od_p, mods_p[5], n_post[1], w_mix_out_bf, tiles_p)
    y_prompt = ffn(xp, mods_p, 2, tiles_p).reshape(bp, lp, D_MODEL)

    xs = ffn(x_sample.reshape(rows_s, D_MODEL), mods_s, 0, 1)
    pos_s = past_len + jnp.arange(ls, dtype=jnp.int32)
    tabs_s = tuple(jnp.tile(t, (bs, 1)) for t in _rope_tables(pos_s))
    qkv_s, z_s, ba_s, qbf_s, k_s, _, v_s, _ = _mix_in(xs, mods_s[3], mods_s[4], n_pre[1], w_mix_in_bf, tabs_s, 1)
    q_rows = _sample_q_rows(qbf_s, bs, ls)
    od_s = _paged_attention(q_rows, _pad_tokens(k_s, bs, ls), _pad_tokens(v_s, bs, ls),
                            cache_k[layer].reshape(-1, cache_k.shape[2], DIFF_QK),
                            cache_v[layer].reshape(-1, cache_v.shape[2], DIFF_V),
                            page_table, lam_vec, d_norm, lam_init, ls)
    od_s = od_s[:, :ls].reshape(rows_s, DIFF_V)
    qkv_s3 = qkv_s.reshape(bs, ls, GDN_CONV_DIM)
    conv_in = jnp.concatenate([state_conv[layer].astype(F32), qkv_s3,
                               jnp.zeros((bs, SAMPLE_PAD - ls - (GDN_CONV - 1), GDN_CONV_DIM), F32)], axis=1)
    og_s, gdn_s = _gdn_sample(conv_in.reshape(bs * SAMPLE_PAD, GDN_CONV_DIM),
                              _pad_tokens(ba_s, bs, ls).reshape(bs * SAMPLE_PAD, LANES),
                              _pad_tokens(z_s, bs, ls).reshape(bs * SAMPLE_PAD, GDN_V),
                              state_gdn[layer], conv_w, a_log, dt_bias, g_norm, ls)
    og_s = og_s.reshape(bs, SAMPLE_PAD, GDN_V)[:, :ls].reshape(rows_s, GDN_V)
    xs = _mix_out(xs, og_s, od_s, mods_s[5], n_post[1], w_mix_out_bf, 1)
    y_sample = ffn(xs, mods_s, 2, 1).reshape(bs, ls, D_MODEL)

    heads = lambda t, b, l: t.reshape(1, b, l, DIFF_HEADS, DIFF_DV)
    return (y_prompt, y_sample, heads(k_p, bp, lp), heads(v_p, bp, lp), heads(k_s, bs, ls), heads(v_s, bs, ls),
            gdn_p[None], gdn_s[None],
            qkv_p.reshape(bp, lp, GDN_CONV_DIM)[:, lp - (GDN_CONV - 1):][None],
            qkv_s3[:, ls - (GDN_CONV - 1):][None])
```
